```python
import math
import jax, jax.numpy as jnp
from jax import lax
import numpy as np

D_MODEL = 1024
BATCH = 4
SEQ = 4096
DEPTH = 2
DEC_BATCH = 32
DEC_SEQ = 1
PAST_LEN = 16384
PAGE_SIZE = 128

HG_HEADS = 8
HG_DK = 128
HG_DV = D_MODEL // HG_HEADS
RET_HEADS = 4
RET_DK = D_MODEL // RET_HEADS
RET_DV = D_MODEL // RET_HEADS
DIFF_HEADS = 8
DIFF_KV_HEADS = 4
DIFF_REP = DIFF_HEADS // DIFF_KV_HEADS
DIFF_HEAD_DIM = D_MODEL // DIFF_HEADS // 2
DIFF_V_DIM = 2 * DIFF_HEAD_DIM
N_BRANCH = 3
BRANCH_W = D_MODEL
D_FF = 4 * D_MODEL
ROPE_THETA = 500000.0
ROT_DIM = DIFF_HEAD_DIM // 4
RET_THETA = 10000.0
CHUNK = 64
Q_BLOCK = 128
EPS = 1e-6
IN_SIZES = (
    HG_HEADS * HG_DK, HG_HEADS * HG_DK, HG_HEADS * HG_DV, HG_HEADS * HG_DV,
    RET_HEADS * RET_DK, RET_HEADS * RET_DK, RET_HEADS * RET_DV, RET_HEADS * RET_DV,
    DIFF_HEADS * 2 * DIFF_HEAD_DIM, DIFF_KV_HEADS * 2 * DIFF_HEAD_DIM, DIFF_KV_HEADS * DIFF_V_DIM,
    N_BRANCH * D_MODEL,
)
IN_COLS = sum(IN_SIZES)

kernel_name = 'hybrid_hgrn2_retention_diffattn_step'


def rms_norm(x, g):
    xf = x.astype(jnp.float32)
    y = xf * lax.rsqrt(jnp.mean(xf * xf, axis=-1, keepdims=True) + EPS)
    return (y * g.astype(jnp.float32)).astype(x.dtype)


def head_layer_norm(x, g):
    xf = x.astype(jnp.float32)
    xc = xf - jnp.mean(xf, axis=-1, keepdims=True)
    y = xc * lax.rsqrt(jnp.mean(xc * xc, axis=-1, keepdims=True) + EPS)
    return (y * g.astype(jnp.float32)).astype(x.dtype)


def partial_rotary(x, pos):
    half = ROT_DIM // 2
    inv = ROPE_THETA ** (-jnp.arange(half, dtype=jnp.float32) / half)
    ang = pos[:, None] * inv[None, :]
    shape = (ang.shape[0],) + (1,) * (x.ndim - 3) + (half,)
    cos = jnp.cos(ang).reshape(shape)
    sin = jnp.sin(ang).reshape(shape)
    xf = x.astype(jnp.float32)
    x1 = xf[..., :half]
    x2 = xf[..., half:ROT_DIM]
    out = jnp.concatenate([x1 * cos - x2 * sin, x2 * cos + x1 * sin, xf[..., ROT_DIM:]], axis=-1)
    return out.astype(x.dtype)


def retention_rotary(x, pos):
    d = x.shape[-1]
    inv = 1.0 / (RET_THETA ** jnp.linspace(0.0, 1.0, d // 2, dtype=jnp.float32))
    ang = pos[:, None] * inv[None, :]
    cos = jnp.cos(ang)[:, None, :]
    sin = jnp.sin(ang)[:, None, :]
    xf = x.astype(jnp.float32)
    x1 = xf[..., 0::2]
    x2 = xf[..., 1::2]
    out = jnp.stack([x1 * cos - x2 * sin, x1 * sin + x2 * cos], axis=-1).reshape(x.shape)
    return out.astype(x.dtype)


def chunk_recurrence(q, k, v, log_f, s0):
    B, L, H, K = q.shape
    V = v.shape[-1]
    C = CHUNK if L % CHUNK == 0 else L
    n = L // C
    per_channel = log_f.shape[-1] > 1

    def blocks(a):
        return a.astype(jnp.float32).reshape(B, n, C, H, a.shape[-1]).transpose(1, 0, 3, 2, 4)

    causal = jnp.tril(jnp.ones((C, C), dtype=bool))[:, :, None]

    def step(S, blk):
        qb, kb, vb, gb = blk
        G = jnp.cumsum(gb, axis=2)
        decay = jnp.exp(jnp.where(causal, G[:, :, :, None, :] - G[:, :, None, :, :], -jnp.inf))
        if per_channel:
            a = jnp.einsum('bhtk,bhsk,bhtsk->bhts', qb, kb, decay)
        else:
            a = jnp.einsum('bhtk,bhsk->bhts', qb, kb) * decay[..., 0]
        o = jnp.einsum('bhts,bhsv->bhtv', a, vb) + jnp.einsum('bhtk,bhkv->bhtv', qb * jnp.exp(G), S)
        g_end = G[:, :, -1:, :]
        S = jnp.exp(g_end[:, :, 0, :])[..., None] * S + jnp.einsum('bhsk,bhsv->bhkv', kb * jnp.exp(g_end - G), vb)
        return S, o

    S, o = lax.scan(step, s0.astype(jnp.float32), (blocks(q), blocks(k), blocks(v), blocks(log_f)))
    o = o.transpose(1, 0, 3, 2, 4).reshape(B, L, H, V)
    return o.astype(v.dtype), S.astype(s0.dtype)


def diff_attn_prompt(q, k, v, lam):
    B, S = q.shape[:2]
    nb = S // Q_BLOCK
    qb = q.reshape((B, nb, Q_BLOCK) + q.shape[2:]).swapaxes(0, 1)
    kpos = jnp.arange(S)
    scale = DIFF_HEAD_DIM ** -0.5

    def block(args):
        qblk, i = args
        s = jnp.einsum('bqgrcd,bkgcd->bgrcqk', qblk, k).astype(jnp.float32) * scale
        qpos = i * Q_BLOCK + jnp.arange(Q_BLOCK)
        s = jnp.where(kpos[None, :] <= qpos[:, None], s, -jnp.inf)
        p = jax.nn.softmax(s, axis=-1)
        a = p[:, :, :, 0] - lam * p[:, :, :, 1]
        return jnp.einsum('bgrqk,bkgv->bqgrv', a.astype(v.dtype), v)

    o = lax.map(block, (qb, jnp.arange(nb)))
    return o.swapaxes(0, 1).reshape((B, S) + o.shape[3:])


def diff_attn_paged(q, k, v, lam, k_pages, v_pages):
    DB, T = q.shape[:2]
    kp = k_pages.reshape((DB, -1) + k_pages.shape[3:])
    vp = v_pages.reshape((DB, -1) + v_pages.shape[3:])
    P = kp.shape[1]
    scale = DIFF_HEAD_DIM ** -0.5
    s_past = jnp.einsum('bqgrcd,bkgcd->bgrcqk', q, kp).astype(jnp.float32)
    s_new = jnp.einsum('bqgrcd,bkgcd->bgrcqk', q, k).astype(jnp.float32)
    s_new = jnp.where(jnp.tril(jnp.ones((T, T), dtype=bool)), s_new, -jnp.inf)
    p = jax.nn.softmax(jnp.concatenate([s_past, s_new], axis=-1) * scale, axis=-1)
    a = (p[:, :, :, 0] - lam * p[:, :, :, 1]).astype(v.dtype)
    return (jnp.einsum('bgrqk,bkgv->bqgrv', a[..., :P], vp)
            + jnp.einsum('bgrqk,bkgv->bqgrv', a[..., P:], v))


def trunk(x, c, pos, attn_fn, s_hg, s_ret, w_ada, b_ada, g_pre_mix, g_post_mix, g_pre_ffn, g_post_ffn,
          w_in, lb_logits, g_hgrn, g_ret, lam_q1, lam_k1, lam_q2, lam_k2, g_diff, w_branch, w_out, w_up, w_down):
    B, L, _ = x.shape
    f32 = jnp.float32
    lb_cum = jnp.cumsum(jax.nn.softmax(lb_logits.astype(f32), axis=0), axis=0)
    lb_all = lb_cum - lb_cum[:1]
    ret_log_decay = jnp.log(1.0 - 2.0 ** (-5.0 - jnp.arange(RET_HEADS, dtype=f32)))
    split_at = np.cumsum(IN_SIZES)[:-1].tolist()
    ks, vs, hgs, rets = [], [], [], []
    for l in range(DEPTH):
        mod = jax.nn.silu(c) @ w_ada[l] + b_ada[l]
        sh1, sc1, ga1, sh2, sc2, ga2 = [m[:, None, :] for m in jnp.split(mod, 6, axis=-1)]
        h = rms_norm(x, g_pre_mix[l]) * (1 + sc1) + sh1
        hq, hf, hi, hg, rq, rk, rv, rg, dq, dk, dv, mg = jnp.split(h @ w_in[l], split_at, axis=-1)
        lb = lb_all[l].reshape(HG_HEADS, HG_DK)
        fr = hf.reshape(B, L, HG_HEADS, HG_DK).astype(f32)
        log_f = jnp.logaddexp(jnp.log(lb), jnp.log1p(-lb) + jax.nn.log_sigmoid(fr))
        k_hg = ((1 - lb) * jax.nn.sigmoid(-fr)).astype(x.dtype)
        q_hg = jax.nn.silu(hq).reshape(B, L, HG_HEADS, HG_DK)
        o_hg, s_hg_l = chunk_recurrence(q_hg, k_hg, hi.reshape(B, L, HG_HEADS, HG_DV), log_f, s_hg[l])
        o_hg = rms_norm(o_hg, g_hgrn[l]) * jax.nn.silu(hg.reshape(B, L, HG_HEADS, HG_DV))
        q_r = retention_rotary(rq.reshape(B, L, RET_HEADS, RET_DK), pos)
        k_r = retention_rotary(rk.reshape(B, L, RET_HEADS, RET_DK), pos) * (RET_DK ** -0.5)
        log_g = jnp.broadcast_to(ret_log_decay[:, None], (B, L, RET_HEADS, 1))
        o_r, s_ret_l = chunk_recurrence(q_r, k_r, rv.reshape(B, L, RET_HEADS, RET_DV), log_g, s_ret[l])
        o_r = head_layer_norm(o_r, g_ret[l]) * jax.nn.silu(rg.reshape(B, L, RET_HEADS, RET_DV))
        q_d = partial_rotary(dq.reshape(B, L, DIFF_KV_HEADS, DIFF_REP, 2, DIFF_HEAD_DIM), pos)
        k_d = partial_rotary(dk.reshape(B, L, DIFF_KV_HEADS, 2, DIFF_HEAD_DIM), pos)
        v_d = dv.reshape(B, L, DIFF_KV_HEADS, DIFF_V_DIM)
        lam_init = 0.8 - 0.6 * math.exp(-0.3 * l)
        lam = (jnp.exp(jnp.sum(lam_q1[l] * lam_k1[l]).astype(f32))
               - jnp.exp(jnp.sum(lam_q2[l] * lam_k2[l]).astype(f32)) + lam_init)
        o_d = attn_fn(l, q_d, k_d, v_d, lam)
        o_d = rms_norm(o_d, g_diff[l]) * (1 - lam_init)
        branches = jnp.stack([o_hg.reshape(B, L, BRANCH_W), o_r.reshape(B, L, BRANCH_W),
                              o_d.reshape(B, L, BRANCH_W)], axis=2)
        yb = jnp.einsum('blnw,nwd->blnd', branches, w_branch[l])
        gates = jax.nn.sigmoid(mg.reshape(B, L, N_BRANCH, D_MODEL))
        out = jnp.sum(gates * yb, axis=2) @ w_out[l]
        x = x + ga1 * rms_norm(out, g_post_mix[l])
        h = rms_norm(x, g_pre_ffn[l]) * (1 + sc2) + sh2
        u = jnp.square(jax.nn.relu(h @ w_up[l]))
        x = x + ga2 * rms_norm(u @ w_down[l], g_post_ffn[l])
        ks.append(k_d)
        vs.append(v_d)
        hgs.append(s_hg_l)
        rets.append(s_ret_l)
    return x, jnp.stack(ks), jnp.stack(vs), jnp.stack(hgs), jnp.stack(rets)


def setup_inputs(seed: int = 0) -> dict:
    key = jax.random.key(seed)
    ks = jax.random.split(key, 40)
    f32 = jnp.float32
    n_pages = PAST_LEN // PAGE_SIZE
    n_pool = (DEC_BATCH * n_pages * 5) // 4

    def nrm(k, shape, scale):
        return jax.random.normal(k, shape, f32) * scale

    def gain(k, shape):
        return 1.0 + 0.02 * jax.random.normal(k, shape, f32)

    page_table = jax.random.permutation(ks[6], n_pool)[: DEC_BATCH * n_pages].reshape(DEC_BATCH, n_pages).astype(jnp.int32)
    return {
        'x_prompt': nrm(ks[0], (BATCH, SEQ, D_MODEL), 1.0),
        'x_sample': nrm(ks[1], (DEC_BATCH, DEC_SEQ, D_MODEL), 1.0),
        'c_prompt': nrm(ks[2], (BATCH, D_MODEL), 1.0),
        'c_sample': nrm(ks[3], (DEC_BATCH, D_MODEL), 1.0),
        'cache_k': nrm(ks[4], (DEPTH, n_pool, PAGE_SIZE, DIFF_KV_HEADS, 2, DIFF_HEAD_DIM), 1.0),
        'cache_v': nrm(ks[5], (DEPTH, n_pool, PAGE_SIZE, DIFF_KV_HEADS, DIFF_V_DIM), 1.0),
        'page_table': page_table,
        'state_hgrn': nrm(ks[7], (DEPTH, DEC_BATCH, HG_HEADS, HG_DK, HG_DV), 1.0),
        'state_ret': nrm(ks[8], (DEPTH, DEC_BATCH, RET_HEADS, RET_DK, RET_DV), 0.5),
        'w_ada': nrm(ks[9], (DEPTH, D_MODEL, 6 * D_MODEL), 0.5 * D_MODEL ** -0.5),
        'b_ada': nrm(ks[10], (DEPTH, 6 * D_MODEL), 0.01),
        'g_pre_mix': gain(ks[11], (DEPTH, D_MODEL)),
        'g_post_mix': gain(ks[12], (DEPTH, D_MODEL)),
        'g_pre_ffn': gain(ks[13], (DEPTH, D_MODEL)),
        'g_post_ffn': gain(ks[14], (DEPTH, D_MODEL)),
        'w_in': nrm(ks[15], (DEPTH, D_MODEL, IN_COLS), D_MODEL ** -0.5),
        'lb_logits': nrm(ks[16], (DEPTH, HG_HEADS * HG_DK), 0.5),
        'g_hgrn': gain(ks[17], (DEPTH, HG_DV)),
        'g_ret': gain(ks[18], (DEPTH, RET_DV)),
        'lam_q1': nrm(ks[19], (DEPTH, DIFF_HEAD_DIM), 0.1),
        'lam_k1': nrm(ks[20], (DEPTH, DIFF_HEAD_DIM), 0.1),
        'lam_q2': nrm(ks[21], (DEPTH, DIFF_HEAD_DIM), 0.1),
        'lam_k2': nrm(ks[22], (DEPTH, DIFF_HEAD_DIM), 0.1),
        'g_diff': gain(ks[23], (DEPTH, DIFF_V_DIM)),
        'w_branch': nrm(ks[24], (DEPTH, N_BRANCH, BRANCH_W, D_MODEL), BRANCH_W ** -0.5),
        'w_out': nrm(ks[25], (DEPTH, D_MODEL, D_MODEL), D_MODEL ** -0.5),
        'w_up': nrm(ks[26], (DEPTH, D_MODEL, D_FF), D_MODEL ** -0.5),
        'w_down': nrm(ks[27], (DEPTH, D_FF, D_MODEL), D_FF ** -0.5),
    }


def reference(x_prompt, x_sample, c_prompt, c_sample, cache_k, cache_v, page_table, state_hgrn, state_ret,
              w_ada, b_ada, g_pre_mix, g_post_mix, g_pre_ffn, g_post_ffn, w_in, lb_logits, g_hgrn, g_ret,
              lam_q1, lam_k1, lam_q2, lam_k2, g_diff, w_branch, w_out, w_up, w_down):
    weights = (w_ada, b_ada, g_pre_mix, g_post_mix, g_pre_ffn, g_post_ffn, w_in, lb_logits, g_hgrn, g_ret,
               lam_q1, lam_k1, lam_q2, lam_k2, g_diff, w_branch, w_out, w_up, w_down)
    B, S, _ = x_prompt.shape
    T = x_sample.shape[1]
    past = page_table.shape[1] * cache_k.shape[2]

    def prompt_attn(l, q, k, v, lam):
        return diff_attn_prompt(q, k, v, lam)

    def sample_attn(l, q, k, v, lam):
        return diff_attn_paged(q, k, v, lam, cache_k[l, page_table], cache_v[l, page_table])

    pos_p = jnp.arange(S, dtype=jnp.float32)
    pos_s = jnp.arange(T, dtype=jnp.float32) + float(past)
    zero_hg = jnp.zeros((DEPTH, B, HG_HEADS, HG_DK, HG_DV), x_prompt.dtype)
    zero_ret = jnp.zeros((DEPTH, B, RET_HEADS, RET_DK, RET_DV), x_prompt.dtype)
    y_prompt, k_prompt, v_prompt, hgrn_prompt, ret_prompt = trunk(
        x_prompt, c_prompt, pos_p, prompt_attn, zero_hg, zero_ret, *weights)
    y_sample, k_sample, v_sample, hgrn_sample, ret_sample = trunk(
        x_sample, c_sample, pos_s, sample_attn, state_hgrn, state_ret, *weights)
    return (y_prompt, y_sample, k_prompt, v_prompt, k_sample, v_sample, hgrn_prompt, hgrn_sample, ret_prompt, ret_sample)
```

```python
import functools
import math

import numpy as np
import jax
import jax.numpy as jnp
from jax import lax
from jax.experimental import pallas as pl
from jax.experimental.pallas import tpu as pltpu

F32 = jnp.float32
BF16 = jnp.bfloat16

D_MODEL = 1024
DEPTH = 2
HG_HEADS = 8
HG_DK = 128
HG_DV = 128
RET_HEADS = 4
RET_DK = 256
RET_DV = 256
DIFF_KV_HEADS = 4
DIFF_REP = 2
DIFF_HEAD_DIM = 64
DIFF_V_DIM = 128
N_BRANCH = 3
D_FF = 4 * D_MODEL
ROPE_THETA = 500000.0
ROT_DIM = DIFF_HEAD_DIM // 4
RET_THETA = 10000.0
EPS = 1e-6
IN_COLS = 13312
COL_HQ, COL_HF, COL_HI, COL_HG = 0, 1, 2, 3
COL_RQ, COL_RK, COL_RV, COL_RG = 4, 5, 6, 7
COL_DQ = 8
COL_DK512, COL_DV512 = 18, 19
COL_MG = 10
MOD_SH1, MOD_SC1, MOD_GA1, MOD_SH2, MOD_SC2, MOD_GA2 = range(6)

HG_CHUNK = 64
RET_CHUNK = 256
ATT_TILE = 512
PAGES_PER_STEP = 8
VMEM_LIMIT = 56 * 1024 * 1024


def _cparams(sem):
    return pltpu.CompilerParams(dimension_semantics=sem, vmem_limit_bytes=VMEM_LIMIT)


def _sigmoid(x):
    return 1.0 / (1.0 + jnp.exp(-x))


def _silu(x):
    return x * _sigmoid(x)


def _rms(x, g):
    return x * lax.rsqrt(jnp.mean(x * x, axis=-1, keepdims=True) + EPS) * g


def _dot(a, b):
    return jnp.dot(a, b, preferred_element_type=F32)


def _dot_nt(a, b):
    return lax.dot_general(a, b, (((1,), (1,)), ((), ())), preferred_element_type=F32)


def _dot_tn(a, b):
    return lax.dot_general(a, b, (((0,), (0,)), ((), ())), preferred_element_type=F32)


def _ada_kernel(c_ref, w_ref, b_ref, o_ref):
    c = c_ref[...]
    o_ref[...] = _dot(_silu(c).astype(BF16), w_ref[...]) + b_ref[...]


def _ada(c_all, w_ada_bf, b_ada):
    rows = c_all.shape[0]
    tn = 1536
    return pl.pallas_call(
        _ada_kernel,
        grid=(DEPTH, 6 * D_MODEL // tn),
        in_specs=[
            pl.BlockSpec((rows, D_MODEL), lambda l, j: (0, 0)),
            pl.BlockSpec((None, D_MODEL, tn), lambda l, j: (l, 0, j)),
            pl.BlockSpec((None, 1, tn), lambda l, j: (l, 0, j)),
        ],
        out_specs=pl.BlockSpec((None, rows, tn), lambda l, j: (l, 0, j)),
        out_shape=jax.ShapeDtypeStruct((DEPTH, rows, 6 * D_MODEL), F32),
        compiler_params=_cparams(("parallel", "parallel")),
        name="ada",
    )(c_all, w_ada_bf, b_ada.reshape(DEPTH, 1, 6 * D_MODEL))


def _inproj_kernel(x_ref, g_ref, sh_ref, sc_ref, w_ref, o_ref, h_ref):
    @pl.when(pl.program_id(2) == 0)
    def _():
        y = _rms(x_ref[...], g_ref[...])
        h_ref[...] = (y * (1.0 + sc_ref[...]) + sh_ref[...]).astype(BF16)

    o_ref[...] = _dot(h_ref[...], w_ref[...])


def _mod_spec(rm, idx, ngrid):
    if ngrid == 3:
        return pl.BlockSpec((None, rm, D_MODEL), lambda b, i, j: (b, 0, idx))
    return pl.BlockSpec((None, rm, D_MODEL), lambda b, i: (b, 0, idx))


def _inproj(x, mod, g_pre, w_in_bf, layer, tm):
    nb, S, _ = x.shape
    rm = mod.shape[1]
    tn = 1024
    return pl.pallas_call(
        _inproj_kernel,
        grid=(nb, S // tm, IN_COLS // tn),
        in_specs=[
            pl.BlockSpec((None, tm, D_MODEL), lambda b, i, j: (b, i, 0)),
            pl.BlockSpec((1, D_MODEL), lambda b, i, j: (0, 0)),
            _mod_spec(rm, MOD_SH1, 3),
            _mod_spec(rm, MOD_SC1, 3),
            pl.BlockSpec((None, D_MODEL, tn), lambda b, i, j: (layer, 0, j)),
        ],
        out_specs=pl.BlockSpec((None, tm, tn), lambda b, i, j: (b, i, j)),
        out_shape=jax.ShapeDtypeStruct((nb, S, IN_COLS), F32),
        scratch_shapes=[pltpu.VMEM((tm, D_MODEL), BF16)],
        compiler_params=_cparams(("parallel", "parallel", "arbitrary")),
        name="inproj",
    )(x, g_pre.reshape(1, D_MODEL), mod, mod, w_in_bf)


def _rot_tables(pos):
    half = ROT_DIM // 2
    inv = ROPE_THETA ** (-jnp.arange(half, dtype=F32) / half)
    ang = pos[:, None] * inv[None, :]
    cos, sin = jnp.cos(ang), jnp.sin(ang)
    n = pos.shape[0]
    one = jnp.ones((n, DIFF_HEAD_DIM - ROT_DIM), F32)
    zero = jnp.zeros((n, DIFF_HEAD_DIM - ROT_DIM), F32)
    zh = jnp.zeros((n, half), F32)
    c64 = jnp.concatenate([cos, cos, one], axis=1)
    sa64 = jnp.concatenate([-sin, zh, zero], axis=1)
    sb64 = jnp.concatenate([zh, sin, zero], axis=1)
    return (jnp.concatenate([c64, c64], axis=1), jnp.concatenate([sa64, sa64], axis=1),
            jnp.concatenate([sb64, sb64], axis=1))


def _rot128(x, c, sa, sb):
    half = ROT_DIM // 2
    return x * c + pltpu.roll(x, 128 - half, 1) * sa + pltpu.roll(x, half, 1) * sb


def _rope_kernel(dq_ref, dk_ref, dv_ref, c_ref, sa_ref, sb_ref, q_ref, k_ref, v_ref, kb_ref, vb_ref):
    c, sa, sb = c_ref[...], sa_ref[...], sb_ref[...]
    scale = DIFF_HEAD_DIM ** -0.5
    for t in range(D_MODEL // 128):
        q_ref[:, t * 128:(t + 1) * 128] = (_rot128(dq_ref[:, t * 128:(t + 1) * 128], c, sa, sb) * scale).astype(BF16)
    for t in range(512 // 128):
        k = _rot128(dk_ref[:, t * 128:(t + 1) * 128], c, sa, sb)
        k_ref[:, t * 128:(t + 1) * 128] = k
        kb_ref[:, t * 128:(t + 1) * 128] = k.astype(BF16)
    v = dv_ref[...]
    v_ref[...] = v
    vb_ref[...] = v.astype(BF16)


def _rope(proj, tabs, ts):
    nb, S, _ = proj.shape
    tr = tabs[0].shape[0]
    tt = ts if tr > 1 else 1
    tab_spec = pl.BlockSpec((tt, 128), (lambda b, i: (i, 0)) if tr > 1 else (lambda b, i: (0, 0)))
    kv_spec = pl.BlockSpec((None, ts, 512), lambda b, i: (b, i, 0))
    return pl.pallas_call(
        _rope_kernel,
        grid=(nb, S // ts),
        in_specs=[
            pl.BlockSpec((None, ts, D_MODEL), lambda b, i: (b, i, COL_DQ)),
            pl.BlockSpec((None, ts, 512), lambda b, i: (b, i, COL_DK512)),
            pl.BlockSpec((None, ts, 512), lambda b, i: (b, i, COL_DV512)),
            tab_spec, tab_spec, tab_spec,
        ],
        out_specs=[pl.BlockSpec((None, ts, D_MODEL), lambda b, i: (b, i, 0)), kv_spec, kv_spec, kv_spec, kv_spec],
        out_shape=[
            jax.ShapeDtypeStruct((nb, S, D_MODEL), BF16),
            jax.ShapeDtypeStruct((nb, S, 512), F32),
            jax.ShapeDtypeStruct((nb, S, 512), F32),
            jax.ShapeDtypeStruct((nb, S, 512), BF16),
            jax.ShapeDtypeStruct((nb, S, 512), BF16),
        ],
        compiler_params=_cparams(("parallel", "parallel")),
        name="rope",
    )(proj, proj, proj, *tabs)


def _hgrn_consts():
    C = HG_CHUNK
    nlev = int(math.log2(C))
    T = np.zeros((nlev + 2, C, C), np.float32)
    for j in range(nlev):
        m = (C // 2) >> j
        for t in range(C):
            bq = (t // (2 * m)) * 2 * m + m - 1
            if t & m:
                T[j, t, bq + 1:t + 1] = 1.0
            else:
                T[j, t, t + 1:bq + 1] = 1.0
    T[nlev] = np.tril(np.ones((C, C), np.float32))
    for s in range(C):
        T[nlev + 1, s, s + 1:C] = 1.0
    lvl = np.full((C, C), -1, np.int32)
    for t in range(C):
        for s in range(t + 1):
            lvl[t, s] = nlev if s == t else (nlev - 1) - int(math.floor(math.log2(t ^ s)))
    return T.reshape((nlev + 2) * C, C), lvl


def _split3(x):
    hi = x.astype(BF16)
    r = x - hi.astype(F32)
    mid = r.astype(BF16)
    lo = (r - mid.astype(F32)).astype(BF16)
    return hi, mid, lo


def _hgrn_gates(fr, lb):
    e = jnp.exp(-jnp.abs(fr))
    log_sig = jnp.minimum(fr, 0.0) - jnp.log1p(e)
    a = jnp.log(lb)
    b = jnp.log1p(-lb) + log_sig
    log_f = jnp.maximum(a, b) + jnp.log1p(jnp.exp(-jnp.abs(a - b)))
    sig_neg = jnp.where(fr >= 0.0, e, 1.0) / (1.0 + e)
    return log_f, (1.0 - lb) * sig_neg


def _hgrn_kernel(q_ref, f_ref, i_ref, g_ref, lb_ref, gn_ref, t_ref, lvl_ref, o_ref, st_ref):
    C = HG_CHUNK
    nlev = int(math.log2(C))

    @pl.when(pl.program_id(1) == 0)
    def _():
        st_ref[...] = jnp.zeros_like(st_ref)

    tmat = t_ref[...]
    lvl = lvl_ref[...]
    for h in range(HG_HEADS):
        sl = slice(h * HG_DK, (h + 1) * HG_DK)
        log_f, kk = _hgrn_gates(f_ref[:, sl], lb_ref[:, sl])
        q = _silu(q_ref[:, sl])
        v = i_ref[:, sl].astype(BF16)
        hi, mid, lo = _split3(log_f)
        x = _dot(tmat, hi) + _dot(tmat, mid) + _dot(tmat, lo)
        a = jnp.zeros((C, C), F32)
        for j in range(nlev):
            e = jnp.exp(x[j * C:(j + 1) * C])
            p = _dot_nt((q * e).astype(BF16), (kk * e).astype(BF16))
            a = jnp.where(lvl == j, p, a)
        a = jnp.where(lvl == nlev, _dot_nt(q.astype(BF16), kk.astype(BF16)), a)
        g = x[nlev * C:(nlev + 1) * C]
        st = st_ref[h]
        o = _dot(a.astype(BF16), v) + _dot_nt((q * jnp.exp(g)).astype(BF16), st.astype(BF16))
        k_end = (kk * jnp.exp(x[(nlev + 1) * C:(nlev + 2) * C])).astype(BF16)
        st_ref[h] = st * jnp.exp(g[C - 1:C, :]) + _dot_tn(v, k_end)
        o_ref[:, sl] = (_rms(o, gn_ref[...]) * _silu(g_ref[:, sl])).astype(BF16)


def _hgrn(proj, lb, g_hgrn):
    nb, S, _ = proj.shape
    C = HG_CHUNK
    tmat, lvl = _hgrn_consts()
    col = lambda idx: pl.BlockSpec((None, C, D_MODEL), lambda b, t: (b, t, idx))
    o, st = pl.pallas_call(
        _hgrn_kernel,
        grid=(nb, S // C),
        in_specs=[
            col(COL_HQ), col(COL_HF), col(COL_HI), col(COL_HG),
            pl.BlockSpec((1, D_MODEL), lambda b, t: (0, 0)),
            pl.BlockSpec((1, HG_DV), lambda b, t: (0, 0)),
            pl.BlockSpec(tmat.shape, lambda b, t: (0, 0)),
            pl.BlockSpec(lvl.shape, lambda b, t: (0, 0)),
        ],
        out_specs=[
            pl.BlockSpec((None, C, D_MODEL), lambda b, t: (b, t, 0)),
            pl.BlockSpec((None, HG_HEADS, HG_DV, HG_DK), lambda b, t: (b, 0, 0, 0)),
        ],
        out_shape=[
            jax.ShapeDtypeStruct((nb, S, D_MODEL), BF16),
            jax.ShapeDtypeStruct((nb, HG_HEADS, HG_DV, HG_DK), F32),
        ],
        compiler_params=_cparams(("parallel", "arbitrary")),
        name="hgrn",
    )(proj, proj, proj, proj, lb.reshape(1, D_MODEL), g_hgrn.reshape(1, HG_DV),
      jnp.asarray(tmat, BF16), jnp.asarray(lvl))
    return o, jnp.swapaxes(st, -1, -2)


def _ret_tables(pos):
    inv = 1.0 / (RET_THETA ** jnp.linspace(0.0, 1.0, RET_DK // 2, dtype=F32))
    ang = pos[:, None] * inv[None, :]
    cos, sin = jnp.cos(ang), jnp.sin(ang)
    c = jnp.stack([cos, cos], axis=-1).reshape(pos.shape[0], RET_DK)
    s = jnp.stack([-sin, sin], axis=-1).reshape(pos.shape[0], RET_DK)
    return c, s


def _ret_rot(x, c, s):
    n = x.shape[-1]
    lane = lax.broadcasted_iota(jnp.int32, x.shape, 1)
    swapped = jnp.where((lane & 1) == 0, pltpu.roll(x, n - 1, 1), pltpu.roll(x, 1, 1))
    return x * c + swapped * s


def _ret_log_decay(h):
    return math.log(1.0 - 2.0 ** (-5.0 - h))


def _head_ln(x, g):
    xc = x - jnp.mean(x, axis=-1, keepdims=True)
    return xc * lax.rsqrt(jnp.mean(xc * xc, axis=-1, keepdims=True) + EPS) * g


def _ret_kernel(q_ref, k_ref, v_ref, g_ref, c_ref, s_ref, gn_ref, o_ref, st_ref):
    C = q_ref.shape[0]

    @pl.when(pl.program_id(1) == 0)
    def _():
        st_ref[...] = jnp.zeros_like(st_ref)

    c, s = c_ref[...], s_ref[...]
    row = lax.broadcasted_iota(jnp.int32, (C, C), 0)
    colm = lax.broadcasted_iota(jnp.int32, (C, C), 1)
    rowk = lax.broadcasted_iota(jnp.int32, (C, RET_DK), 0).astype(F32)
    for h in range(RET_HEADS):
        sl = slice(h * RET_DK, (h + 1) * RET_DK)
        lg = _ret_log_decay(h)
        q = _ret_rot(q_ref[:, sl], c, s)
        k = _ret_rot(k_ref[:, sl], c, s) * (RET_DK ** -0.5)
        v = v_ref[:, sl].astype(BF16)
        dmat = jnp.where(row >= colm, jnp.exp((row - colm).astype(F32) * lg), 0.0)
        a = _dot_nt(q.astype(BF16), k.astype(BF16)) * dmat
        st = st_ref[h]
        o = _dot(a.astype(BF16), v) + _dot((q * jnp.exp((rowk + 1.0) * lg)).astype(BF16), st.astype(BF16))
        k_end = (k * jnp.exp((C - 1.0 - rowk) * lg)).astype(BF16)
        st_ref[h] = st * math.exp(C * lg) + _dot_tn(k_end, v)
        o_ref[:, sl] = (_head_ln(o, gn_ref[...]) * _silu(g_ref[:, sl])).astype(BF16)


def _ret(proj, tabs, g_ret):
    nb, S, _ = proj.shape
    C = min(RET_CHUNK, S)
    col = lambda idx: pl.BlockSpec((None, C, D_MODEL), lambda b, t: (b, t, idx))
    tab = pl.BlockSpec((C, RET_DK), lambda b, t: (t, 0))
    return pl.pallas_call(
        _ret_kernel,
        grid=(nb, S // C),
        in_specs=[col(COL_RQ), col(COL_RK), col(COL_RV), col(COL_RG), tab, tab,
                  pl.BlockSpec((1, RET_DV), lambda b, t: (0, 0))],
        out_specs=[
            pl.BlockSpec((None, C, D_MODEL), lambda b, t: (b, t, 0)),
            pl.BlockSpec((None, RET_HEADS, RET_DK, RET_DV), lambda b, t: (b, 0, 0, 0)),
        ],
        out_shape=[
            jax.ShapeDtypeStruct((nb, S, D_MODEL), BF16),
            jax.ShapeDtypeStruct((nb, RET_HEADS, RET_DK, RET_DV), F32),
        ],
        compiler_params=_cparams(("parallel", "arbitrary")),
        name="ret",
    )(proj, proj, proj, proj, tabs[0], tabs[1], g_ret.reshape(1, RET_DV))


def _lam(lam_ref, lam_init):
    lv = lam_ref[...]
    return (jnp.exp(jnp.sum(lv[0:1] * lv[1:2], axis=1, keepdims=True))
            - jnp.exp(jnp.sum(lv[2:3] * lv[3:4], axis=1, keepdims=True)) + lam_init)


def _attn_kernel(q_ref, k_ref, v_ref, lam_ref, gd_ref, o_ref, m_ref, l_ref, acc_ref, *, lam_init):
    tq = q_ref.shape[0]
    tk = k_ref.shape[0]
    qi = pl.program_id(2)
    ki = pl.program_id(3)

    @pl.when(ki == 0)
    def _():
        m_ref[...] = jnp.full_like(m_ref, -jnp.inf)
        l_ref[...] = jnp.zeros_like(l_ref)
        acc_ref[...] = jnp.zeros_like(acc_ref)

    def step(masked):
        k = k_ref[...]
        v = v_ref[...]
        lane = lax.broadcasted_iota(jnp.int32, (tq, 2 * DIFF_HEAD_DIM), 1)
        if masked:
            row = lax.broadcasted_iota(jnp.int32, (tq, tk), 0)
            col = lax.broadcasted_iota(jnp.int32, (tq, tk), 1)
        for r in range(DIFF_REP):
            qr = q_ref[:, r * 128:(r + 1) * 128]
            for c in range(2):
                vh = r * 2 + c
                qc = jnp.where((lane < DIFF_HEAD_DIM) == (c == 0), qr, jnp.zeros_like(qr))
                s = _dot_nt(qc, k)
                if masked:
                    s = jnp.where(col <= row, s, -jnp.inf)
                m_prev = m_ref[vh][:, :1]
                m_new = jnp.maximum(m_prev, jnp.max(s, axis=1, keepdims=True))
                alpha = jnp.exp(m_prev - m_new)
                p = jnp.exp(s - m_new)
                l_new = alpha * l_ref[vh][:, :1] + jnp.sum(p, axis=1, keepdims=True)
                acc_ref[vh] = alpha * acc_ref[vh] + _dot(p.astype(BF16), v)
                m_ref[vh] = jnp.broadcast_to(m_new, (tq, 128))
                l_ref[vh] = jnp.broadcast_to(l_new, (tq, 128))

    @pl.when(ki < qi)
    def _():
        step(False)

    @pl.when(ki == qi)
    def _():
        step(True)
        lam = _lam(lam_ref, lam_init)
        for r in range(DIFF_REP):
            o0 = acc_ref[2 * r] / l_ref[2 * r][:, :1]
            o1 = acc_ref[2 * r + 1] / l_ref[2 * r + 1][:, :1]
            d = o0 - lam * o1
            o_ref[:, r * 128:(r + 1) * 128] = (_rms(d, gd_ref[...]) * (1.0 - lam_init)).astype(BF16)


def _attn(q_bf, k_bf, v_bf, lam_vecs, g_diff, lam_init):
    nb, S, _ = q_bf.shape
    t = min(ATT_TILE, S)
    n = S // t
    kv_spec = pl.BlockSpec((None, t, 128), lambda b, g, qi, ki: (b, jnp.minimum(ki, qi), g))
    return pl.pallas_call(
        functools.partial(_attn_kernel, lam_init=lam_init),
        grid=(nb, DIFF_KV_HEADS, n, n),
        in_specs=[
            pl.BlockSpec((None, t, 256), lambda b, g, qi, ki: (b, qi, g)),
            kv_spec, kv_spec,
            pl.BlockSpec((4, DIFF_HEAD_DIM), lambda b, g, qi, ki: (0, 0)),
            pl.BlockSpec((1, DIFF_V_DIM), lambda b, g, qi, ki: (0, 0)),
        ],
        out_specs=pl.BlockSpec((None, t, 256), lambda b, g, qi, ki: (b, qi, g)),
        out_shape=jax.ShapeDtypeStruct((nb, S, D_MODEL), BF16),
        scratch_shapes=[pltpu.VMEM((4, t, 128), F32), pltpu.VMEM((4, t, 128), F32), pltpu.VMEM((4, t, 128), F32)],
        compiler_params=_cparams(("parallel", "parallel", "parallel", "arbitrary")),
        name="attn",
    )(q_bf, k_bf, v_bf, lam_vecs, g_diff.reshape(1, DIFF_V_DIM))


def _merge_kernel(x_ref, b0_ref, b1_ref, b2_ref, m0_ref, m1_ref, m2_ref, wb_ref, wo_ref, g_ref, ga_ref, o_ref):
    acc = None
    for n, (b_ref, m_ref) in enumerate(((b0_ref, m0_ref), (b1_ref, m1_ref), (b2_ref, m2_ref))):
        t = _sigmoid(m_ref[...]) * _dot(b_ref[...], wb_ref[n])
        acc = t if acc is None else acc + t
    out = _dot(acc.astype(BF16), wo_ref[...])
    o_ref[...] = x_ref[...] + ga_ref[...] * _rms(out, g_ref[...])


def _merge(x, br, proj, mod, w_branch_bf, w_out_bf, g_post, layer, tm):
    nb, S, _ = x.shape
    rm = mod.shape[1]
    row = pl.BlockSpec((None, tm, D_MODEL), lambda b, i: (b, i, 0))
    mg = lambda n: pl.BlockSpec((None, tm, D_MODEL), lambda b, i: (b, i, COL_MG + n))
    return pl.pallas_call(
        _merge_kernel,
        grid=(nb, S // tm),
        in_specs=[
            row, row, row, row, mg(0), mg(1), mg(2),
            pl.BlockSpec((None, N_BRANCH, D_MODEL, D_MODEL), lambda b, i: (layer, 0, 0, 0)),
            pl.BlockSpec((None, D_MODEL, D_MODEL), lambda b, i: (layer, 0, 0)),
            pl.BlockSpec((1, D_MODEL), lambda b, i: (0, 0)),
            _mod_spec(rm, MOD_GA1, 2),
        ],
        out_specs=row,
        out_shape=jax.ShapeDtypeStruct((nb, S, D_MODEL), F32),
        compiler_params=_cparams(("parallel", "parallel")),
        name="merge",
    )(x, br[0], br[1], br[2], proj, proj, proj, w_branch_bf, w_out_bf, g_post.reshape(1, D_MODEL), mod)


def _mlp_kernel(x_ref, g_ref, sh_ref, sc_ref, wu_ref, wd_ref, gp_ref, ga_ref, o_ref, h_ref, acc_ref):
    kf = pl.program_id(2)

    @pl.when(kf == 0)
    def _():
        y = _rms(x_ref[...], g_ref[...])
        h_ref[...] = (y * (1.0 + sc_ref[...]) + sh_ref[...]).astype(BF16)
        acc_ref[...] = jnp.zeros_like(acc_ref)

    u = jnp.square(jnp.maximum(_dot(h_ref[...], wu_ref[...]), 0.0))
    acc_ref[...] += _dot(u.astype(BF16), wd_ref[...])

    @pl.when(kf == pl.num_programs(2) - 1)
    def _():
        o_ref[...] = x_ref[...] + ga_ref[...] * _rms(acc_ref[...], gp_ref[...])


def _mlp(x, mod, g_pre, w_up_bf, w_down_bf, g_post, layer, tm):
    nb, S, _ = x.shape
    rm = mod.shape[1]
    tf = 1024
    row = pl.BlockSpec((None, tm, D_MODEL), lambda b, i, j: (b, i, 0))
    vec = pl.BlockSpec((1, D_MODEL), lambda b, i, j: (0, 0))
    return pl.pallas_call(
        _mlp_kernel,
        grid=(nb, S // tm, D_FF // tf),
        in_specs=[
            row, vec, _mod_spec(rm, MOD_SH2, 3), _mod_spec(rm, MOD_SC2, 3),
            pl.BlockSpec((None, D_MODEL, tf), lambda b, i, j: (layer, 0, j)),
            pl.BlockSpec((None, tf, D_MODEL), lambda b, i, j: (layer, j, 0)),
            vec, _mod_spec(rm, MOD_GA2, 3),
        ],
        out_specs=row,
        out_shape=jax.ShapeDtypeStruct((nb, S, D_MODEL), F32),
        scratch_shapes=[pltpu.VMEM((tm, D_MODEL), BF16), pltpu.VMEM((tm, D_MODEL), F32)],
        compiler_params=_cparams(("parallel", "parallel", "arbitrary")),
        name="mlp",
    )(x, g_pre.reshape(1, D_MODEL), mod, mod, w_up_bf, w_down_bf, g_post.reshape(1, D_MODEL), mod)


def _col(rows):
    n = rows[0].shape[1]
    return [jnp.transpose(jnp.broadcast_to(r, (n, n))) for r in rows]


def _hgrn_step_kernel(q_ref, f_ref, i_ref, g_ref, lb_ref, gn_ref, s0_ref, o_ref, s1_ref):
    for h in range(HG_HEADS):
        sl = slice(h * HG_DK, (h + 1) * HG_DK)
        log_f, kk = _hgrn_gates(f_ref[:, sl], lb_ref[:, sl])
        f = jnp.exp(log_f)
        q = _silu(q_ref[:, sl])
        v = i_ref[:, sl]
        s0 = s0_ref[h]
        qf_c, f_c, k_c = _col([q * f, f, kk])
        o = jnp.sum(qf_c * s0, axis=0, keepdims=True) + jnp.sum(q * kk, axis=1, keepdims=True) * v
        s1_ref[h] = f_c * s0 + k_c * v
        o_ref[:, sl] = (_rms(o, gn_ref[...]) * _silu(g_ref[:, sl])).astype(BF16)


def _hgrn_step(proj, lb, g_hgrn, s0):
    nb = proj.shape[0]
    col = lambda idx: pl.BlockSpec((None, 1, D_MODEL), lambda b: (b, 0, idx))
    st = pl.BlockSpec((None, HG_HEADS, HG_DK, HG_DV), lambda b: (b, 0, 0, 0))
    return pl.pallas_call(
        _hgrn_step_kernel,
        grid=(nb,),
        in_specs=[col(COL_HQ), col(COL_HF), col(COL_HI), col(COL_HG),
                  pl.BlockSpec((1, D_MODEL), lambda b: (0, 0)),
                  pl.BlockSpec((1, HG_DV), lambda b: (0, 0)), st],
        out_specs=[pl.BlockSpec((None, 1, D_MODEL), lambda b: (b, 0, 0)), st],
        out_shape=[jax.ShapeDtypeStruct((nb, 1, D_MODEL), BF16), jax.ShapeDtypeStruct(s0.shape, F32)],
        compiler_params=_cparams(("parallel",)),
        name="hgrn_step",
    )(proj, proj, proj, proj, lb.reshape(1, D_MODEL), g_hgrn.reshape(1, HG_DV), s0)


def _ret_step_kernel(q_ref, k_ref, v_ref, g_ref, c_ref, s_ref, gn_ref, s0_ref, o_ref, s1_ref):
    c, s = c_ref[...], s_ref[...]
    for h in range(RET_HEADS):
        sl = slice(h * RET_DK, (h + 1) * RET_DK)
        gamma = math.exp(_ret_log_decay(h))
        q = _ret_rot(q_ref[:, sl], c, s)
        k = _ret_rot(k_ref[:, sl], c, s) * (RET_DK ** -0.5)
        v = v_ref[:, sl]
        s0 = s0_ref[h]
        q_c, k_c = _col([q * gamma, k])
        o = jnp.sum(q_c * s0, axis=0, keepdims=True) + jnp.sum(q * k, axis=1, keepdims=True) * v
        s1_ref[h] = gamma * s0 + k_c * v
        o_ref[:, sl] = (_head_ln(o, gn_ref[...]) * _silu(g_ref[:, sl])).astype(BF16)


def _ret_step(proj, tabs, g_ret, s0):
    nb = proj.shape[0]
    col = lambda idx: pl.BlockSpec((None, 1, D_MODEL), lambda b: (b, 0, idx))
    st = pl.BlockSpec((None, RET_HEADS, RET_DK, RET_DV), lambda b: (b, 0, 0, 0))
    tab = pl.BlockSpec((1, RET_DK), lambda b: (0, 0))
    return pl.pallas_call(
        _ret_step_kernel,
        grid=(nb,),
        in_specs=[col(COL_RQ), col(COL_RK), col(COL_RV), col(COL_RG), tab, tab,
                  pl.BlockSpec((1, RET_DV), lambda b: (0, 0)), st],
        out_specs=[pl.BlockSpec((None, 1, D_MODEL), lambda b: (b, 0, 0)), st],
        out_shape=[jax.ShapeDtypeStruct((nb, 1, D_MODEL), BF16), jax.ShapeDtypeStruct(s0.shape, F32)],
        compiler_params=_cparams(("parallel",)),
        name="ret_step",
    )(proj, proj, proj, proj, tabs[0], tabs[1], g_ret.reshape(1, RET_DV), s0)


NVH = DIFF_KV_HEADS * DIFF_REP * 2


def _paged_kernel(pt_ref, q_ref, kn_ref, vn_ref, lam_ref, gd_ref, *refs, lam_init, pps):
    k_refs = refs[:pps]
    v_refs = refs[pps:2 * pps]
    o_ref, qm_ref, m_ref, l_ref, acc_ref = refs[2 * pps:]
    j = pl.program_id(1)
    W = DIFF_KV_HEADS * 128

    @pl.when(j == 0)
    def _():
        q = q_ref[...].astype(F32)
        q_r = [jnp.concatenate([q[:, g * 256 + r * 128: g * 256 + (r + 1) * 128] for g in range(DIFF_KV_HEADS)], axis=1)
               for r in range(DIFF_REP)]
        i = lax.broadcasted_iota(jnp.int32, (NVH, W), 0)
        col = lax.broadcasted_iota(jnp.int32, (NVH, W), 1)
        src = jnp.where((i & 1) == 0, jnp.broadcast_to(q_r[0], (NVH, W)), jnp.broadcast_to(q_r[1], (NVH, W)))
        keep = ((col >> 7) == ((i >> 1) & 3)) & (((col >> 6) & 1) == (i >> 3))
        qm_ref[...] = jnp.where(keep, src, 0.0)
        m_ref[...] = jnp.full_like(m_ref, -jnp.inf)
        l_ref[...] = jnp.zeros_like(l_ref)
        acc_ref[...] = jnp.zeros_like(acc_ref)

    qm = qm_ref[...].astype(BF16)
    s = jnp.concatenate([_dot(qm, k_refs[p][...].astype(BF16)) for p in range(pps)], axis=1)
    m_prev = m_ref[:, :1]
    m_new = jnp.maximum(m_prev, jnp.max(s, axis=1, keepdims=True))
    alpha = jnp.exp(m_prev - m_new)
    p = jnp.exp(s - m_new)
    l_ref[...] = jnp.broadcast_to(alpha * l_ref[:, :1] + jnp.sum(p, axis=1, keepdims=True), l_ref.shape)
    m_ref[...] = jnp.broadcast_to(m_new, m_ref.shape)
    pb = p.astype(BF16)
    gi = (lax.broadcasted_iota(jnp.int32, (NVH, 128), 0) >> 1) & 3
    npos = k_refs[0].shape[1]
    upd = jnp.zeros((NVH, 128), F32)
    for g in range(DIFF_KV_HEADS):
        pv = None
        for pg in range(pps):
            v_g = v_refs[pg][pl.ds(g, npos, stride=DIFF_KV_HEADS), :].astype(BF16)
            t = _dot(pb[:, pg * npos:(pg + 1) * npos], v_g)
            pv = t if pv is None else pv + t
        upd = jnp.where(gi == g, pv, upd)
    acc_ref[...] = alpha * acc_ref[...] + upd

    @pl.when(j == pl.num_programs(1) - 1)
    def _():
        s_new = jnp.sum(qm_ref[...] * kn_ref[...], axis=1, keepdims=True)
        m_prev = m_ref[:, :1]
        m_fin = jnp.maximum(m_prev, s_new)
        alpha = jnp.exp(m_prev - m_fin)
        p_new = jnp.exp(s_new - m_fin)
        l_fin = alpha * l_ref[:, :1] + p_new
        vn = jnp.zeros((NVH, 128), F32)
        for g in range(DIFF_KV_HEADS):
            vn = jnp.where(gi == g, jnp.broadcast_to(vn_ref[:, g * 128:(g + 1) * 128], (NVH, 128)), vn)
        o = (alpha * acc_ref[...] + p_new * vn) / l_fin
        lam = _lam(lam_ref, lam_init)
        d = o[0:8] - lam * o[8:16]
        o_ref[...] = (_rms(d, gd_ref[...]) * (1.0 - lam_init)).astype(BF16)


def _paged_attn(page_table, q_bf, k_new, v_new, cache_k, cache_v, layer, lam_vecs, g_diff, lam_init):
    nb, n_pages = page_table.shape
    pps = PAGES_PER_STEP
    W = DIFF_KV_HEADS * 128

    def page_spec(p):
        return pl.BlockSpec((None, None, W, cache_k.shape[3]), lambda b, j, pt: (layer, pt[b, j * pps + p], 0, 0))

    row = lambda w: pl.BlockSpec((None, 1, w), lambda b, j, pt: (b, 0, 0))
    out = pl.pallas_call(
        functools.partial(_paged_kernel, lam_init=lam_init, pps=pps),
        grid_spec=pltpu.PrefetchScalarGridSpec(
            num_scalar_prefetch=1,
            grid=(nb, n_pages // pps),
            in_specs=[row(D_MODEL), row(W), row(W),
                      pl.BlockSpec((4, DIFF_HEAD_DIM), lambda b, j, pt: (0, 0)),
                      pl.BlockSpec((1, DIFF_V_DIM), lambda b, j, pt: (0, 0))]
                     + [page_spec(p) for p in range(pps)] + [page_spec(p) for p in range(pps)],
            out_specs=pl.BlockSpec((None, 8, 128), lambda b, j, pt: (b, 0, 0)),
            scratch_shapes=[pltpu.VMEM((NVH, W), F32), pltpu.VMEM((NVH, 128), F32),
                            pltpu.VMEM((NVH, 128), F32), pltpu.VMEM((NVH, 128), F32)],
        ),
        out_shape=jax.ShapeDtypeStruct((nb, 8, 128), BF16),
        compiler_params=_cparams(("parallel", "arbitrary")),
        name="paged_attn",
    )(page_table, q_bf, k_new, v_new, lam_vecs, g_diff.reshape(1, DIFF_V_DIM),
      *([cache_k] * pps), *([cache_v] * pps))
    return out.reshape(nb, 1, D_MODEL)


def _lower_bounds(lb_logits):
    lb_cum = jnp.cumsum(jax.nn.softmax(lb_logits.astype(F32), axis=0), axis=0)
    return lb_cum - lb_cum[:1]


def kernel(x_prompt, x_sample, c_prompt, c_sample, cache_k, cache_v, page_table, state_hgrn, state_ret, w_ada, b_ada, g_pre_mix, g_post_mix, g_pre_ffn, g_post_ffn, w_in, lb_logits, g_hgrn, g_ret, lam_q1, lam_k1, lam_q2, lam_k2, g_diff, w_branch, w_out, w_up, w_down):
    B, S, _ = x_prompt.shape
    DB = x_sample.shape[0]
    past = page_table.shape[1] * cache_k.shape[2]

    w_ada_bf, w_in_bf, w_branch_bf = w_ada.astype(BF16), w_in.astype(BF16), w_branch.astype(BF16)
    w_out_bf, w_up_bf, w_down_bf = w_out.astype(BF16), w_up.astype(BF16), w_down.astype(BF16)

    n_c = B + DB
    pad = (-n_c) % 8
    c_all = jnp.concatenate([c_prompt, c_sample, jnp.zeros((pad, D_MODEL), F32)], axis=0)
    mod = _ada(c_all, w_ada_bf, b_ada)
    lb_all = _lower_bounds(lb_logits)

    pos_p = jnp.arange(S, dtype=F32)
    pos_s = jnp.full((1,), float(past), F32)
    rot_p, rot_s = _rot_tables(pos_p), _rot_tables(pos_s)
    ret_p, ret_s = _ret_tables(pos_p), _ret_tables(pos_s)
    n_l, n_pool, page = cache_k.shape[:3]
    ck = jnp.transpose(cache_k, (0, 1, 3, 4, 5, 2)).reshape(n_l, n_pool, -1, page)
    cv = cache_v.reshape(n_l, n_pool, page * DIFF_KV_HEADS, DIFF_V_DIM)

    xp = x_prompt
    xs = x_sample.reshape(1, DB, D_MODEL)
    outs = {k: [] for k in ("kp", "vp", "ks", "vs", "hp", "hs", "rp", "rs")}
    for l in range(DEPTH):
        lam_init = 0.8 - 0.6 * math.exp(-0.3 * l)
        lam_vecs = jnp.stack([lam_q1[l], lam_k1[l], lam_q2[l], lam_k2[l]])
        mod_p = mod[l, :B].reshape(B, 1, 6 * D_MODEL)
        mod_s = mod[l, B:B + DB].reshape(1, DB, 6 * D_MODEL)

        proj = _inproj(xp, mod_p, g_pre_mix[l], w_in_bf, l, tm=min(1024, S))
        q_bf, k_f, v_f, k_bf, v_bf = _rope(proj, rot_p, ts=min(512, S))
        o_hg, st_hg = _hgrn(proj, lb_all[l], g_hgrn[l])
        o_ret, st_ret = _ret(proj, ret_p, g_ret[l])
        o_d = _attn(q_bf, k_bf, v_bf, lam_vecs, g_diff[l], lam_init)
        xp = _merge(xp, (o_hg, o_ret, o_d), proj, mod_p, w_branch_bf, w_out_bf, g_post_mix[l], l, tm=min(256, S))
        xp = _mlp(xp, mod_p, g_pre_ffn[l], w_up_bf, w_down_bf, g_post_ffn[l], l, tm=min(512, S))
        outs["kp"].append(k_f.reshape(B, S, DIFF_KV_HEADS, 2, DIFF_HEAD_DIM))
        outs["vp"].append(v_f.reshape(B, S, DIFF_KV_HEADS, DIFF_V_DIM))
        outs["hp"].append(st_hg)
        outs["rp"].append(st_ret)

        proj = _inproj(xs, mod_s, g_pre_mix[l], w_in_bf, l, tm=DB)
        q_bf, k_f, v_f, _, _ = _rope(proj, rot_s, ts=DB)
        proj_r = proj.reshape(DB, 1, IN_COLS)
        o_hg, st_hg = _hgrn_step(proj_r, lb_all[l], g_hgrn[l], state_hgrn[l])
        o_ret, st_ret = _ret_step(proj_r, ret_s, g_ret[l], state_ret[l])
        o_d = _paged_attn(page_table, q_bf.reshape(DB, 1, D_MODEL), k_f.reshape(DB, 1, 512), v_f.reshape(DB, 1, 512),
                          ck, cv, l, lam_vecs, g_diff[l], lam_init)
        br = (o_hg.reshape(1, DB, D_MODEL), o_ret.reshape(1, DB, D_MODEL), o_d.reshape(1, DB, D_MODEL))
        xs = _merge(xs, br, proj, mod_s, w_branch_bf, w_out_bf, g_post_mix[l], l, tm=DB)
        xs = _mlp(xs, mod_s, g_pre_ffn[l], w_up_bf, w_down_bf, g_post_ffn[l], l, tm=DB)
        outs["ks"].append(k_f.reshape(DB, 1, DIFF_KV_HEADS, 2, DIFF_HEAD_DIM))
        outs["vs"].append(v_f.reshape(DB, 1, DIFF_KV_HEADS, DIFF_V_DIM))
        outs["hs"].append(st_hg)
        outs["rs"].append(st_ret)

    st = lambda k: jnp.stack(outs[k])
    return (xp, xs.reshape(DB, 1, D_MODEL), st("kp"), st("vp"), st("ks"), st("vs"),
            st("hp"), st("hs"), st("rp"), st("rs"))
```
